```python
import jax, jax.numpy as jnp
from jax import lax
import numpy as np

D_MODEL = 1024
BATCH = 8
SEQ = 2048
DEPTH = 2
DEC_BATCH = 128
DEC_SEQ = 1
PAST_LEN = 16384
PAGE_SIZE = 128

W_CONF = D_MODEL // 2
CONF_K = 31
POOL_WINDOWS = (2, 4, 8, 16)
N_POOL_GROUPS = 4
W_POOL = D_MODEL // 2
POOL_GW = W_POOL // N_POOL_GROUPS
POOL_OUT_GW = D_MODEL // N_POOL_GROUPS
POOL_BUF = 15
W_SC = D_MODEL // 2
SC_K = 3
N_BRANCH = 3
SPLIT_SIZES = (W_CONF, W_CONF, W_POOL, W_SC, W_SC, W_SC, D_MODEL, D_MODEL, D_MODEL)
D_IN = 2 * W_CONF + W_POOL + 3 * W_SC + N_BRANCH * D_MODEL
N_GROUPS = 4
EXP_PER_GROUP = 4
N_EXPERTS = N_GROUPS * EXP_PER_GROUP
TOP_K_INNER = 2
D_FF_EXPERT = D_MODEL // 2
EPS = 1e-6

kernel_name = "hybrid_conv_pool_shortconv_hmoe_step"


def _rmsnorm(x, g):
    xf = x.astype(jnp.float32)
    y = xf * lax.rsqrt(jnp.mean(xf * xf, axis=-1, keepdims=True) + EPS)
    return (y * g.astype(jnp.float32)).astype(x.dtype)


def _layernorm(x, g, b):
    xf = x.astype(jnp.float32)
    mu = jnp.mean(xf, axis=-1, keepdims=True)
    var = jnp.mean(jnp.square(xf - mu), axis=-1, keepdims=True)
    y = (xf - mu) * lax.rsqrt(var + EPS)
    return (y * g.astype(jnp.float32) + b.astype(jnp.float32)).astype(x.dtype)


def _causal_dwconv(u, buf, w):
    k = w.shape[0]
    full = jnp.concatenate([buf.astype(u.dtype), u], axis=1)
    y = lax.conv_general_dilated(full, w.astype(u.dtype)[:, None, :], window_strides=(1,),
                                 padding='VALID', dimension_numbers=('NWC', 'WIO', 'NWC'),
                                 feature_group_count=u.shape[-1])
    return y, full[:, full.shape[1] - (k - 1):]


def _multiscale_pool(u, buf, pos0, pool_w, pool_scale):
    t = u.shape[1]
    full = jnp.concatenate([buf.astype(u.dtype), u], axis=1)
    fullf = full.astype(jnp.float32)
    cs = jnp.concatenate([jnp.zeros_like(fullf[:, :1]), lax.cumsum(fullf, axis=1)], axis=1)
    pos = pos0 + jnp.arange(t)
    off = POOL_BUF + 1
    outs = []
    for gi, w in enumerate(POOL_WINDOWS):
        sl = slice(gi * POOL_GW, (gi + 1) * POOL_GW)
        win_sum = cs[:, off:off + t, sl] - cs[:, off - w:off - w + t, sl]
        count = jnp.minimum(pos + 1, w).astype(jnp.float32)
        pooled = (win_sum / count[None, :, None]).astype(u.dtype) - u[..., sl]
        outs.append(jnp.einsum('btc,co->bto', pooled, pool_w[gi]))
    y = jnp.concatenate(outs, axis=-1) * pool_scale
    return y, full[:, full.shape[1] - POOL_BUF:]


def _mixer(h, buf_a, buf_b, buf_c, pos0, w_in, conf_dw, conf_dw_b, conf_ln_g, conf_ln_b,
           conf_pw, pool_w, pool_scale, sc_dw, sc_out, w_o):
    z = jnp.einsum('btd,de->bte', h, w_in)
    idx = [int(i) for i in np.cumsum(SPLIT_SIZES)[:-1]]
    a_val, a_gate, u_b, s_x, s_b, s_c, z_ga, z_gb, z_gc = jnp.split(z, idx, axis=-1)
    u_a = a_val * jax.nn.sigmoid(a_gate)
    v_a, nb_a = _causal_dwconv(u_a, buf_a, conf_dw)
    v_a = v_a + conf_dw_b
    y_a = jnp.einsum('btc,cd->btd', jax.nn.silu(_layernorm(v_a, conf_ln_g, conf_ln_b)), conf_pw)
    y_b, nb_b = _multiscale_pool(u_b, buf_b, pos0, pool_w, pool_scale)
    v_c, nb_c = _causal_dwconv(s_c * s_x, buf_c, sc_dw)
    y_c = jnp.einsum('btc,cd->btd', s_b * v_c, sc_out)
    m = jax.nn.sigmoid(z_ga) * y_a + jax.nn.sigmoid(z_gb) * y_b + jax.nn.sigmoid(z_gc) * y_c
    return jnp.einsum('btd,de->bte', m, w_o), nb_a, nb_b, nb_c


def _moe(h, rg, rg_b, re, re_b, w1, w3, w2):
    b, t, d = h.shape
    hf = h.reshape(b * t, d)
    gp = jax.nn.softmax((hf @ rg + rg_b).astype(jnp.float32), axis=-1)
    gidx = jnp.argmax(gp, axis=-1)
    gval = jnp.take_along_axis(gp, gidx[:, None], axis=1)[:, 0]
    el = (hf @ re + re_b).astype(jnp.float32).reshape(-1, N_GROUPS, EXP_PER_GROUP)
    sel = jnp.take_along_axis(el, gidx[:, None, None], axis=1)[:, 0]
    tv, ti = lax.top_k(jax.nn.softmax(sel, axis=-1), TOP_K_INNER)
    tv = tv / jnp.sum(tv, axis=-1, keepdims=True)
    w_inner = jnp.sum(jax.nn.one_hot(ti, EXP_PER_GROUP, dtype=jnp.float32) * tv[..., None], axis=1)
    comb = (jax.nn.one_hot(gidx, N_GROUPS, dtype=jnp.float32)[:, :, None]
            * (gval[:, None, None] * w_inner[:, None, :])).reshape(-1, N_EXPERTS).astype(h.dtype)
    y = jnp.zeros_like(hf)
    for e in range(N_EXPERTS):
        act = jax.nn.silu(hf @ w1[e]) * (hf @ w3[e])
        y = y + (act @ w2[e]) * comb[:, e:e + 1]
    return y.reshape(b, t, d)


def _run_group(x, c, st_a, st_b, st_c, pos0, P):
    new_a, new_b, new_c = [], [], []
    sc = jax.nn.silu(c)
    for l in range(DEPTH):
        mod = sc @ P['w_ada'][l] + P['b_ada'][l]
        sh1, s1, g1, sh2, s2, g2 = jnp.split(mod[:, None, :], 6, axis=-1)
        h = _rmsnorm(x, P['norm1_g'][l]) * (1 + s1) + sh1
        m, na, nb, nc = _mixer(h, st_a[l], st_b[l], st_c[l], pos0, P['w_in'][l], P['conf_dw'][l],
                               P['conf_dw_b'][l], P['conf_ln_g'][l], P['conf_ln_b'][l],
                               P['conf_pw'][l], P['pool_w'][l], P['pool_scale'][l],
                               P['sc_dw'][l], P['sc_out'][l], P['w_o'][l])
        x = x + g1 * m
        h = _rmsnorm(x, P['norm2_g'][l]) * (1 + s2) + sh2
        x = x + g2 * _moe(h, P['router_g'][l], P['router_g_b'][l], P['router_e'][l],
                          P['router_e_b'][l], P['exp_w1'][l], P['exp_w3'][l], P['exp_w2'][l])
        new_a.append(na); new_b.append(nb); new_c.append(nc)
    y = _rmsnorm(x, P['final_g'])
    return y, jnp.stack(new_a, 0), jnp.stack(new_b, 0), jnp.stack(new_c, 0)


def setup_inputs(seed: int = 0) -> dict:
    key = jax.random.key(seed)
    ks = jax.random.split(key, 32)
    f = jnp.float32
    nrm = lambda k, s, sc: jax.random.normal(k, s, f) * sc
    D = D_MODEL
    return {
        'x_prompt': nrm(ks[0], (BATCH, SEQ, D), 1.0),
        'x_sample': nrm(ks[1], (DEC_BATCH, DEC_SEQ, D), 1.0),
        'state_conf': nrm(ks[2], (DEPTH, DEC_BATCH, CONF_K - 1, W_CONF), 1.0),
        'state_pool': nrm(ks[3], (DEPTH, DEC_BATCH, POOL_BUF, W_POOL), 1.0),
        'state_sconv': nrm(ks[4], (DEPTH, DEC_BATCH, SC_K - 1, W_SC), 1.0),
        'c_prompt': nrm(ks[5], (BATCH, D), 1.0),
        'c_sample': nrm(ks[6], (DEC_BATCH, D), 1.0),
        'w_ada': nrm(ks[7], (DEPTH, D, 6 * D), 0.5 * D ** -0.5),
        'b_ada': nrm(ks[8], (DEPTH, 6 * D), 0.01),
        'norm1_g': 1.0 + nrm(ks[9], (DEPTH, D), 0.02),
        'norm2_g': 1.0 + nrm(ks[10], (DEPTH, D), 0.02),
        'final_g': 1.0 + nrm(ks[11], (D,), 0.02),
        'w_in': nrm(ks[12], (DEPTH, D, D_IN), D ** -0.5),
        'conf_dw': nrm(ks[13], (DEPTH, CONF_K, W_CONF), CONF_K ** -0.5),
        'conf_dw_b': nrm(ks[14], (DEPTH, W_CONF), 0.01),
        'conf_ln_g': 1.0 + nrm(ks[15], (DEPTH, W_CONF), 0.02),
        'conf_ln_b': nrm(ks[16], (DEPTH, W_CONF), 0.02),
        'conf_pw': nrm(ks[17], (DEPTH, W_CONF, D), W_CONF ** -0.5),
        'pool_w': nrm(ks[18], (DEPTH, N_POOL_GROUPS, POOL_GW, POOL_OUT_GW), POOL_GW ** -0.5),
        'pool_scale': 1.0 + nrm(ks[19], (DEPTH, D), 0.1),
        'sc_dw': nrm(ks[20], (DEPTH, SC_K, W_SC), SC_K ** -0.5),
        'sc_out': nrm(ks[21], (DEPTH, W_SC, D), W_SC ** -0.5),
        'w_o': nrm(ks[22], (DEPTH, D, D), D ** -0.5),
        'router_g': nrm(ks[23], (DEPTH, D, N_GROUPS), D ** -0.5),
        'router_g_b': nrm(ks[24], (DEPTH, N_GROUPS), 0.01),
        'router_e': nrm(ks[25], (DEPTH, D, N_EXPERTS), D ** -0.5),
        'router_e_b': nrm(ks[26], (DEPTH, N_EXPERTS), 0.01),
        'exp_w1': nrm(ks[27], (DEPTH, N_EXPERTS, D, D_FF_EXPERT), D ** -0.5),
        'exp_w3': nrm(ks[28], (DEPTH, N_EXPERTS, D, D_FF_EXPERT), D ** -0.5),
        'exp_w2': nrm(ks[29], (DEPTH, N_EXPERTS, D_FF_EXPERT, D), D_FF_EXPERT ** -0.5),
    }


def reference(x_prompt, x_sample, state_conf, state_pool, state_sconv, c_prompt, c_sample,
              w_ada, b_ada, norm1_g, norm2_g, final_g, w_in, conf_dw, conf_dw_b, conf_ln_g,
              conf_ln_b, conf_pw, pool_w, pool_scale, sc_dw, sc_out, w_o, router_g, router_g_b,
              router_e, router_e_b, exp_w1, exp_w3, exp_w2):
    P = dict(w_ada=w_ada, b_ada=b_ada, norm1_g=norm1_g, norm2_g=norm2_g, final_g=final_g,
             w_in=w_in, conf_dw=conf_dw, conf_dw_b=conf_dw_b, conf_ln_g=conf_ln_g,
             conf_ln_b=conf_ln_b, conf_pw=conf_pw, pool_w=pool_w, pool_scale=pool_scale,
             sc_dw=sc_dw, sc_out=sc_out, w_o=w_o, router_g=router_g, router_g_b=router_g_b,
             router_e=router_e, router_e_b=router_e_b, exp_w1=exp_w1, exp_w3=exp_w3, exp_w2=exp_w2)
    dt = x_prompt.dtype
    z_a = jnp.zeros((DEPTH, BATCH, CONF_K - 1, W_CONF), dt)
    z_b = jnp.zeros((DEPTH, BATCH, POOL_BUF, W_POOL), dt)
    z_c = jnp.zeros((DEPTH, BATCH, SC_K - 1, W_SC), dt)
    y_prompt, conf_p, pool_p, sconv_p = _run_group(x_prompt, c_prompt, z_a, z_b, z_c, 0, P)
    y_sample, conf_s, pool_s, sconv_s = _run_group(x_sample, c_sample, state_conf, state_pool,
                                                    state_sconv, PAST_LEN, P)
    return (y_prompt, y_sample, conf_p, conf_s, pool_p, pool_s, sconv_p, sconv_s)
```

```python
import functools

import jax
import jax.numpy as jnp
from jax import lax
from jax.experimental import pallas as pl
from jax.experimental.pallas import tpu as pltpu

F32 = jnp.float32
BF16 = jnp.bfloat16

EPS = 1e-6
CONF_K = 31
POOL_WINDOWS = (2, 4, 8, 16)
POOL_BUF = 15
SC_K = 3
N_GROUPS = 4
EXP_PER_GROUP = 4
N_EXPERTS = N_GROUPS * EXP_PER_GROUP
PAST_LEN = 16384

V7X_VMEM_LIMIT_BYTES = 56 * 1024 * 1024
LANES = 128
HALO_A = 32
HALO_B = 16
HALO_C = 8
CONV_ROW_CHUNK = 32


def _sigmoid(v):
    return jax.nn.sigmoid(v)


def _rms_mod(x, g, scale, shift):
    ms = jnp.mean(x * x, axis=-1, keepdims=True)
    return (x * lax.rsqrt(ms + EPS) * g) * (1.0 + scale) + shift


def _bdot(a, b):
    return jnp.dot(a, b, preferred_element_type=F32)


def _modulation_kernel(c_ref, w_ref, b_ref, o_ref):
    c = c_ref[...]
    sc = c * _sigmoid(c)
    o_ref[...] = jnp.dot(sc, w_ref[...], precision=lax.Precision.HIGHEST,
                         preferred_element_type=F32) + b_ref[...]


def _modulation(c_all, w_ada, b_ada):
    depth, d, n = w_ada.shape
    rows = c_all.shape[0]
    bn = 512
    return pl.pallas_call(
        _modulation_kernel,
        grid=(depth, n // bn),
        in_specs=[
            pl.BlockSpec((rows, d), lambda l, j: (0, 0)),
            pl.BlockSpec((None, d, bn), lambda l, j: (l, 0, j)),
            pl.BlockSpec((None, 1, bn), lambda l, j: (l, 0, j)),
        ],
        out_specs=pl.BlockSpec((None, rows, bn), lambda l, j: (l, 0, j)),
        out_shape=jax.ShapeDtypeStruct((depth, rows, n), F32),
        compiler_params=pltpu.CompilerParams(
            dimension_semantics=("arbitrary", "arbitrary"),
            vmem_limit_bytes=V7X_VMEM_LIMIT_BYTES),
        name="adaln_modulation",
    )(c_all, w_ada, b_ada.reshape(depth, 1, n))


def _mixer_tail(x, mod, hb, y_a_in, pooled, sc_in, w):
    d = x.shape[-1]
    (win_ref, cpw_ref, pw_ref, ps_ref, sout_ref, wo_ref) = w
    y_a = _bdot(y_a_in.astype(BF16), cpw_ref[...])
    gw = pooled.shape[-1] // len(POOL_WINDOWS)
    y_b = jnp.concatenate(
        [_bdot(pooled[:, gi * gw:(gi + 1) * gw].astype(BF16), pw_ref[gi])
         for gi in range(len(POOL_WINDOWS))], axis=-1) * ps_ref[...]
    y_c = _bdot(sc_in.astype(BF16), sout_ref[...])
    c0 = win_ref.shape[-1] - 3 * d
    m = _sigmoid(_bdot(hb, win_ref[:, c0:c0 + d])) * y_a
    m = m + _sigmoid(_bdot(hb, win_ref[:, c0 + d:c0 + 2 * d])) * y_b
    m = m + _sigmoid(_bdot(hb, win_ref[:, c0 + 2 * d:c0 + 3 * d])) * y_c
    return x + mod[:, 2 * d:3 * d] * _bdot(m.astype(BF16), wo_ref[...])


def _layernorm_silu(v, g, b):
    mu = jnp.mean(v, axis=-1, keepdims=True)
    vc = v - mu
    var = jnp.mean(vc * vc, axis=-1, keepdims=True)
    y = vc * lax.rsqrt(var + EPS) * g + b
    return y * _sigmoid(y)


def _mixer_prompt_kernel(x_ref, mod_ref, g1_ref, win_ref, cdw_ref, cdwb_ref, lng_ref, lnb_ref,
                         cpw_ref, pw_ref, ps_ref, sdw_ref, sout_ref, wo_ref,
                         xo_ref, nconf_ref, npool_ref, nsc_ref,
                         hist_a, hist_b, hist_c, *, tt):
    t = pl.program_id(1)
    d = x_ref.shape[-1]
    wc = hist_a.shape[-1]

    @pl.when(t == 0)
    def _():
        hist_a[0:HALO_A, :] = jnp.zeros((HALO_A, wc), F32)
        hist_b[0:HALO_B, :] = jnp.zeros((HALO_B, wc), F32)
        hist_c[0:HALO_C, :] = jnp.zeros((HALO_C, wc), F32)

    x = x_ref[...]
    mod = mod_ref[...]
    hb = _rms_mod(x, g1_ref[...], mod[:, d:2 * d], mod[:, 0:d]).astype(BF16)

    za = _bdot(hb, win_ref[:, 0:2 * wc])
    hist_a[HALO_A:HALO_A + tt, :] = za[:, :wc] * _sigmoid(za[:, wc:])
    base_a = HALO_A - (CONF_K - 1)
    chunks = []
    for r0 in range(0, tt, CONV_ROW_CHUNK):
        acc = None
        for k in range(CONF_K):
            term = hist_a[pl.ds(base_a + r0 + k, CONV_ROW_CHUNK), :] * cdw_ref[k:k + 1, :]
            acc = term if acc is None else acc + term
        chunks.append(acc)
    v_a = jnp.concatenate(chunks, axis=0) + cdwb_ref[...]
    y_a_in = _layernorm_silu(v_a, lng_ref[...], lnb_ref[...])

    u_b = _bdot(hb, win_ref[:, 2 * wc:3 * wc])
    hist_b[HALO_B:HALO_B + tt, :] = u_b
    pos = t * tt + lax.broadcasted_iota(jnp.int32, (tt, 1), 0)
    gw = wc // len(POOL_WINDOWS)
    pooled = []
    for gi, win in enumerate(POOL_WINDOWS):
        sl = slice(gi * gw, (gi + 1) * gw)
        s = u_b[:, sl]
        for j in range(1, win):
            s = s + hist_b[pl.ds(HALO_B - j, tt), sl]
        count = jnp.minimum(pos + 1, win).astype(F32)
        pooled.append(s / count - u_b[:, sl])
    pooled = jnp.concatenate(pooled, axis=-1)

    zs = _bdot(hb, win_ref[:, 3 * wc:6 * wc])
    s_x, s_b, s_c = zs[:, 0:wc], zs[:, wc:2 * wc], zs[:, 2 * wc:3 * wc]
    hist_c[HALO_C:HALO_C + tt, :] = s_c * s_x
    base_c = HALO_C - (SC_K - 1)
    v_c = None
    for k in range(SC_K):
        term = hist_c[pl.ds(base_c + k, tt), :] * sdw_ref[k:k + 1, :]
        v_c = term if v_c is None else v_c + term

    xo_ref[...] = _mixer_tail(x, mod, hb, y_a_in, pooled, s_b * v_c,
                              (win_ref, cpw_ref, pw_ref, ps_ref, sout_ref, wo_ref))

    @pl.when(t == pl.num_programs(1) - 1)
    def _():
        nconf_ref[...] = hist_a[pl.ds(HALO_A + tt - (CONF_K - 1), CONF_K - 1), :]
        npool_ref[...] = hist_b[pl.ds(HALO_B + tt - POOL_BUF, POOL_BUF), :]
        nsc_ref[...] = hist_c[pl.ds(HALO_C + tt - (SC_K - 1), SC_K - 1), :]

    hist_a[0:HALO_A, :] = hist_a[tt:tt + HALO_A, :]
    hist_b[0:HALO_B, :] = hist_b[tt:tt + HALO_B, :]
    hist_c[0:HALO_C, :] = hist_c[tt:tt + HALO_C, :]


def _const_spec(shape, index):
    return pl.BlockSpec(shape, index, pipeline_mode=pl.Buffered(1))


def _mixer_weight_specs(l, w, grid_rank):
    zeros = lambda n: (0,) * n
    specs = []
    for a in w:
        nd = a.ndim - 1
        idx = (lambda *_, nd=nd: (l,) + zeros(nd))
        specs.append(_const_spec((None,) + a.shape[1:], idx))
    return specs


def _mixer_prompt(l, x, mod4, row0, g1, w, tt):
    b, t, d = x.shape
    win, cdw, cdwb, lng, lnb, cpw, pw, ps, sdw, sout, wo = w
    wc = cdw.shape[-1]
    kern = functools.partial(_mixer_prompt_kernel, tt=tt)
    weights = (g1, win, cdw, cdwb, lng, lnb, cpw, pw, ps, sdw, sout, wo)
    return pl.pallas_call(
        kern,
        grid=(b, t // tt),
        in_specs=[
            pl.BlockSpec((None, tt, d), lambda i, j: (i, j, 0)),
            pl.BlockSpec((None, None, 1, 3 * d), lambda i, j: (l, row0 + i, 0, 0)),
        ] + _mixer_weight_specs(l, weights, 2),
        out_specs=[
            pl.BlockSpec((None, tt, d), lambda i, j: (i, j, 0)),
            pl.BlockSpec((None, CONF_K - 1, wc), lambda i, j: (i, 0, 0)),
            pl.BlockSpec((None, POOL_BUF, wc), lambda i, j: (i, 0, 0)),
            pl.BlockSpec((None, SC_K - 1, wc), lambda i, j: (i, 0, 0)),
        ],
        out_shape=[
            jax.ShapeDtypeStruct((b, t, d), F32),
            jax.ShapeDtypeStruct((b, CONF_K - 1, wc), F32),
            jax.ShapeDtypeStruct((b, POOL_BUF, wc), F32),
            jax.ShapeDtypeStruct((b, SC_K - 1, wc), F32),
        ],
        scratch_shapes=[
            pltpu.VMEM((HALO_A + tt, wc), F32),
            pltpu.VMEM((HALO_B + tt, wc), F32),
            pltpu.VMEM((HALO_C + tt, wc), F32),
        ],
        compiler_params=pltpu.CompilerParams(
            dimension_semantics=("arbitrary", "arbitrary"),
            vmem_limit_bytes=V7X_VMEM_LIMIT_BYTES),
        name=f"mixer_prompt_l{l}",
    )(x, mod4, *weights)


def _mixer_sample_kernel(x_ref, mod_ref, sta_ref, stb_ref, stc_ref, g1_ref, win_ref, cdw_ref,
                         cdwb_ref, lng_ref, lnb_ref, cpw_ref, pw_ref, ps_ref, sdw_ref, sout_ref,
                         wo_ref, xo_ref, nconf_ref, npool_ref, nsc_ref):
    d = x_ref.shape[-1]
    wc = sta_ref.shape[-1]
    x = x_ref[...]
    mod = mod_ref[...]
    hb = _rms_mod(x, g1_ref[...], mod[:, d:2 * d], mod[:, 0:d]).astype(BF16)

    za = _bdot(hb, win_ref[:, 0:2 * wc])
    u_a = za[:, :wc] * _sigmoid(za[:, wc:])
    v_a = u_a * cdw_ref[CONF_K - 1:CONF_K, :]
    for k in range(CONF_K - 1):
        v_a = v_a + sta_ref[:, k, :] * cdw_ref[k:k + 1, :]
    y_a_in = _layernorm_silu(v_a + cdwb_ref[...], lng_ref[...], lnb_ref[...])
    for k in range(CONF_K - 2):
        nconf_ref[:, k, :] = sta_ref[:, k + 1, :]
    nconf_ref[:, CONF_K - 2, :] = u_a

    u_b = _bdot(hb, win_ref[:, 2 * wc:3 * wc])
    gw = wc // len(POOL_WINDOWS)
    pooled = []
    for gi, win in enumerate(POOL_WINDOWS):
        sl = slice(gi * gw, (gi + 1) * gw)
        s = u_b[:, sl]
        for j in range(1, win):
            s = s + stb_ref[:, POOL_BUF - j, sl]
        count = float(min(PAST_LEN + 1, win))
        pooled.append(s / count - u_b[:, sl])
    pooled = jnp.concatenate(pooled, axis=-1)
    for k in range(POOL_BUF - 1):
        npool_ref[:, k, :] = stb_ref[:, k + 1, :]
    npool_ref[:, POOL_BUF - 1, :] = u_b

    zs = _bdot(hb, win_ref[:, 3 * wc:6 * wc])
    s_x, s_b, s_c = zs[:, 0:wc], zs[:, wc:2 * wc], zs[:, 2 * wc:3 * wc]
    gated = s_c * s_x
    v_c = gated * sdw_ref[SC_K - 1:SC_K, :]
    for k in range(SC_K - 1):
        v_c = v_c + stc_ref[:, k, :] * sdw_ref[k:k + 1, :]
    for k in range(SC_K - 2):
        nsc_ref[:, k, :] = stc_ref[:, k + 1, :]
    nsc_ref[:, SC_K - 2, :] = gated

    xo_ref[...] = _mixer_tail(x, mod, hb, y_a_in, pooled, s_b * v_c,
                              (win_ref, cpw_ref, pw_ref, ps_ref, sout_ref, wo_ref))


def _mixer_sample(l, x, mod, sta, stb, stc, g1, w):
    n, d = x.shape
    win, cdw, cdwb, lng, lnb, cpw, pw, ps, sdw, sout, wo = w
    wc = cdw.shape[-1]
    weights = (g1, win, cdw, cdwb, lng, lnb, cpw, pw, ps, sdw, sout, wo)
    state_spec = lambda a: pl.BlockSpec((None,) + a.shape[1:], lambda i: (l, 0, 0, 0))
    return pl.pallas_call(
        _mixer_sample_kernel,
        grid=(1,),
        in_specs=[
            pl.BlockSpec((n, d), lambda i: (0, 0)),
            pl.BlockSpec((None, n, 3 * d), lambda i: (l, 0, 0)),
            state_spec(sta), state_spec(stb), state_spec(stc),
        ] + _mixer_weight_specs(l, weights, 1),
        out_specs=[
            pl.BlockSpec((n, d), lambda i: (0, 0)),
            pl.BlockSpec((n, CONF_K - 1, wc), lambda i: (0, 0, 0)),
            pl.BlockSpec((n, POOL_BUF, wc), lambda i: (0, 0, 0)),
            pl.BlockSpec((n, SC_K - 1, wc), lambda i: (0, 0, 0)),
        ],
        out_shape=[
            jax.ShapeDtypeStruct((n, d), F32),
            jax.ShapeDtypeStruct((n, CONF_K - 1, wc), F32),
            jax.ShapeDtypeStruct((n, POOL_BUF, wc), F32),
            jax.ShapeDtypeStruct((n, SC_K - 1, wc), F32),
        ],
        compiler_params=pltpu.CompilerParams(
            dimension_semantics=("arbitrary",),
            vmem_limit_bytes=V7X_VMEM_LIMIT_BYTES),
        name=f"mixer_sample_l{l}",
    )(x, mod, sta, stb, stc, *weights)


def _route(logits):
    lane = lax.broadcasted_iota(jnp.int32, logits.shape, 1)
    neg = -jnp.inf
    is_g = lane < N_GROUPS
    gl = jnp.where(is_g, logits, neg)
    gmax = jnp.max(gl, axis=-1, keepdims=True)
    gsum = jnp.sum(jnp.where(is_g, jnp.exp(gl - gmax), 0.0), axis=-1, keepdims=True)
    gval = 1.0 / gsum
    gidx = jnp.min(jnp.where(gl == gmax, lane, LANES), axis=-1, keepdims=True)
    in_grp = (lane >= N_GROUPS) & (lane < N_GROUPS + N_EXPERTS) & (
        (lane - N_GROUPS) // EXP_PER_GROUP == gidx)
    sel = jnp.where(in_grp, logits, neg)
    smax = jnp.max(sel, axis=-1, keepdims=True)
    sexp = jnp.where(in_grp, jnp.exp(sel - smax), 0.0)
    p = sexp / jnp.sum(sexp, axis=-1, keepdims=True)
    p1 = jnp.max(jnp.where(in_grp, p, -1.0), axis=-1, keepdims=True)
    i1 = jnp.min(jnp.where(in_grp & (p == p1), lane, LANES), axis=-1, keepdims=True)
    rest = in_grp & (lane != i1)
    p2 = jnp.max(jnp.where(rest, p, -1.0), axis=-1, keepdims=True)
    i2 = jnp.min(jnp.where(rest & (p == p2), lane, LANES), axis=-1, keepdims=True)
    tsum = p1 + p2
    return jnp.where(lane == i1, gval * (p1 / tsum),
                     jnp.where(lane == i2, gval * (p2 / tsum), 0.0))


def _moe_dense_kernel(x_ref, mod_ref, g2_ref, rw_ref, rb_ref, w1_ref, w3_ref, w2_ref, fg_ref,
                      o_ref, hb_ref, comb_ref, acc_ref, *, final_norm):
    e = pl.program_id(2)
    d = x_ref.shape[-1]

    @pl.when(e == 0)
    def _():
        mod = mod_ref[...]
        h = _rms_mod(x_ref[...], g2_ref[...], mod[:, d:2 * d], mod[:, 0:d])
        hb_ref[...] = h.astype(BF16)
        logits = jnp.dot(h, rw_ref[...], precision=lax.Precision.HIGHEST,
                         preferred_element_type=F32) + rb_ref[...]
        comb_ref[...] = _route(logits)
        acc_ref[...] = jnp.zeros(acc_ref.shape, F32)

    hb = hb_ref[...]
    a = _bdot(hb, w1_ref[...])
    act = (a * _sigmoid(a)) * _bdot(hb, w3_ref[...])
    y = _bdot(act.astype(BF16), w2_ref[...])
    comb = comb_ref[...]
    lane = lax.broadcasted_iota(jnp.int32, comb.shape, 1)
    ce = jnp.sum(jnp.where(lane == e + N_GROUPS, comb, 0.0), axis=-1, keepdims=True)
    acc_ref[...] += y * ce

    @pl.when(e == pl.num_programs(2) - 1)
    def _():
        xn = x_ref[...] + mod_ref[:, 2 * d:3 * d] * acc_ref[...]
        if final_norm:
            ms = jnp.mean(xn * xn, axis=-1, keepdims=True)
            xn = xn * lax.rsqrt(ms + EPS) * fg_ref[...]
        o_ref[...] = xn


def _moe_dense(l, x, mod_spec, mod_arr, g2, rw, rb, w1, w3, w2, fg, tm, final_norm):
    b, t, d = x.shape
    ne, _, dff = w1.shape[1:]
    kern = functools.partial(_moe_dense_kernel, final_norm=final_norm)
    return pl.pallas_call(
        kern,
        grid=(b, t // tm, ne),
        in_specs=[
            pl.BlockSpec((None, tm, d), lambda i, j, e: (i, j, 0)),
            mod_spec,
            _const_spec((None, 1, d), lambda i, j, e: (l, 0, 0)),
            _const_spec((None, d, LANES), lambda i, j, e: (l, 0, 0)),
            _const_spec((None, 1, LANES), lambda i, j, e: (l, 0, 0)),
            pl.BlockSpec((None, None, d, dff), lambda i, j, e: (l, e, 0, 0)),
            pl.BlockSpec((None, None, d, dff), lambda i, j, e: (l, e, 0, 0)),
            pl.BlockSpec((None, None, dff, d), lambda i, j, e: (l, e, 0, 0)),
            _const_spec((1, d), lambda i, j, e: (0, 0)),
        ],
        out_specs=pl.BlockSpec((None, tm, d), lambda i, j, e: (i, j, 0)),
        out_shape=jax.ShapeDtypeStruct((b, t, d), F32),
        scratch_shapes=[
            pltpu.VMEM((tm, d), BF16),
            pltpu.VMEM((tm, LANES), F32),
            pltpu.VMEM((tm, d), F32),
        ],
        compiler_params=pltpu.CompilerParams(
            dimension_semantics=("arbitrary", "arbitrary", "arbitrary"),
            vmem_limit_bytes=V7X_VMEM_LIMIT_BYTES),
        name=f"moe_dense_l{l}_{'prompt' if b > 1 else 'sample'}",
    )(x, mod_arr, g2, rw, rb, w1, w3, w2, fg)


MIXER_TIME_TILE = 256
MOE_TOKEN_TILE = 1024


def kernel(x_prompt, x_sample, state_conf, state_pool, state_sconv, c_prompt, c_sample, w_ada, b_ada, norm1_g, norm2_g, final_g, w_in, conf_dw, conf_dw_b, conf_ln_g, conf_ln_b, conf_pw, pool_w, pool_scale, sc_dw, sc_out, w_o, router_g, router_g_b, router_e, router_e_b, exp_w1, exp_w3, exp_w2):
    depth, d, _ = w_ada.shape
    nb, seq, _ = x_prompt.shape
    ns = x_sample.shape[0]

    c_all = jnp.concatenate([c_sample, c_prompt], axis=0)
    mod = _modulation(c_all, w_ada, b_ada)
    mod4 = mod.reshape(depth, ns + nb, 1, 6 * d)

    row = lambda a: a.reshape(depth, 1, a.shape[-1])
    mixer_w = (w_in.astype(BF16), conf_dw, row(conf_dw_b), row(conf_ln_g), row(conf_ln_b),
               conf_pw.astype(BF16), pool_w.astype(BF16), row(pool_scale), sc_dw,
               sc_out.astype(BF16), w_o.astype(BF16))
    g1, g2 = row(norm1_g), row(norm2_g)
    fg = final_g.reshape(1, d)
    pad = LANES - N_GROUPS - N_EXPERTS
    rw = jnp.pad(jnp.concatenate([router_g, router_e], axis=-1), ((0, 0), (0, 0), (0, pad)))
    rb = jnp.pad(jnp.concatenate([router_g_b, router_e_b], axis=-1), ((0, 0), (0, pad)))
    rb = rb.reshape(depth, 1, LANES)
    w1, w3, w2 = exp_w1.astype(BF16), exp_w3.astype(BF16), exp_w2.astype(BF16)

    xp = x_prompt
    xs = x_sample.reshape(ns, d)
    new_p = [[], [], []]
    new_s = [[], [], []]
    for l in range(depth):
        last = l == depth - 1
        xp, na, nbuf, nc = _mixer_prompt(l, xp, mod4, ns, g1, mixer_w, MIXER_TIME_TILE)
        for dst, v in zip(new_p, (na, nbuf, nc)):
            dst.append(v)
        mod_spec_p = pl.BlockSpec((None, None, 1, 3 * d),
                                  lambda i, j, e, l=l: (l, ns + i, 0, 1))
        xp = _moe_dense(l, xp, mod_spec_p, mod4, g2, rw, rb, w1, w3, w2, fg,
                        MOE_TOKEN_TILE, last)
        xs, na, nbuf, nc = _mixer_sample(l, xs, mod, state_conf, state_pool, state_sconv,
                                         g1, mixer_w)
        for dst, v in zip(new_s, (na, nbuf, nc)):
            dst.append(v)
        mod_spec_s = pl.BlockSpec((None, ns, 3 * d), lambda i, j, e, l=l: (l, 0, 1))
        xs = _moe_dense(l, xs.reshape(1, ns, d), mod_spec_s, mod, g2, rw, rb, w1, w3, w2, fg,
                        ns, last).reshape(ns, d)

    stack = lambda parts: jnp.stack(parts, axis=0)
    return (xp, xs.reshape(ns, 1, d),
            stack(new_p[0]), stack(new_s[0]),
            stack(new_p[1]), stack(new_s[1]),
            stack(new_p[2]), stack(new_s[2]))
```

```python
import functools

import jax
import jax.numpy as jnp
from jax import lax
from jax.experimental import pallas as pl
from jax.experimental.pallas import tpu as pltpu
from jax.experimental.pallas import tpu_sc as plsc

F32 = jnp.float32
BF16 = jnp.bfloat16

EPS = 1e-6
CONF_K = 31
POOL_WINDOWS = (2, 4, 8, 16)
POOL_BUF = 15
SC_K = 3
N_GROUPS = 4
EXP_PER_GROUP = 4
N_EXPERTS = N_GROUPS * EXP_PER_GROUP
PAST_LEN = 16384

V7X_VMEM_LIMIT_BYTES = 56 * 1024 * 1024
LANES = 128
HALO_A = 32
HALO_B = 16
HALO_C = 8
CONV_ROW_CHUNK = 32


def _sigmoid(v):
    return jax.nn.sigmoid(v)


def _rms_mod(x, g, scale, shift):
    ms = jnp.mean(x * x, axis=-1, keepdims=True)
    return (x * lax.rsqrt(ms + EPS) * g) * (1.0 + scale) + shift


def _bdot(a, b):
    return jnp.dot(a, b, preferred_element_type=F32)


def _modulation_kernel(c_ref, w_ref, b_ref, o_ref):
    c = c_ref[...]
    sc = c * _sigmoid(c)
    o_ref[...] = jnp.dot(sc, w_ref[...], precision=lax.Precision.HIGHEST,
                         preferred_element_type=F32) + b_ref[...]


def _modulation(c_all, w_ada, b_ada):
    depth, d, n = w_ada.shape
    rows = c_all.shape[0]
    bn = 512
    return pl.pallas_call(
        _modulation_kernel,
        grid=(depth, n // bn),
        in_specs=[
            pl.BlockSpec((rows, d), lambda l, j: (0, 0)),
            pl.BlockSpec((None, d, bn), lambda l, j: (l, 0, j)),
            pl.BlockSpec((None, 1, bn), lambda l, j: (l, 0, j)),
        ],
        out_specs=pl.BlockSpec((None, rows, bn), lambda l, j: (l, 0, j)),
        out_shape=jax.ShapeDtypeStruct((depth, rows, n), F32),
        compiler_params=pltpu.CompilerParams(
            dimension_semantics=("arbitrary", "arbitrary"),
            vmem_limit_bytes=V7X_VMEM_LIMIT_BYTES),
        name="adaln_modulation",
    )(c_all, w_ada, b_ada.reshape(depth, 1, n))


def _mixer_tail(x, mod, hb, y_a_in, pooled, sc_in, w):
    d = x.shape[-1]
    (win_ref, cpw_ref, pw_ref, ps_ref, sout_ref, wo_ref) = w
    y_a = _bdot(y_a_in.astype(BF16), cpw_ref[...])
    gw = pooled.shape[-1] // len(POOL_WINDOWS)
    y_b = jnp.concatenate(
        [_bdot(pooled[:, gi * gw:(gi + 1) * gw].astype(BF16), pw_ref[gi])
         for gi in range(len(POOL_WINDOWS))], axis=-1) * ps_ref[...]
    y_c = _bdot(sc_in.astype(BF16), sout_ref[...])
    c0 = win_ref.shape[-1] - 3 * d
    m = _sigmoid(_bdot(hb, win_ref[:, c0:c0 + d])) * y_a
    m = m + _sigmoid(_bdot(hb, win_ref[:, c0 + d:c0 + 2 * d])) * y_b
    m = m + _sigmoid(_bdot(hb, win_ref[:, c0 + 2 * d:c0 + 3 * d])) * y_c
    return x + mod[:, 2 * d:3 * d] * _bdot(m.astype(BF16), wo_ref[...])


def _layernorm_silu(v, g, b):
    mu = jnp.mean(v, axis=-1, keepdims=True)
    vc = v - mu
    var = jnp.mean(vc * vc, axis=-1, keepdims=True)
    y = vc * lax.rsqrt(var + EPS) * g + b
    return y * _sigmoid(y)


def _mixer_prompt_kernel(x_ref, mod_ref, g1_ref, win_ref, cdw_ref, cdwb_ref, lng_ref, lnb_ref,
                         cpw_ref, pw_ref, ps_ref, sdw_ref, sout_ref, wo_ref,
                         xo_ref, nconf_ref, npool_ref, nsc_ref,
                         hist_a, hist_b, hist_c, *, tt):
    t = pl.program_id(1)
    d = x_ref.shape[-1]
    wc = hist_a.shape[-1]

    @pl.when(t == 0)
    def _():
        hist_a[0:HALO_A, :] = jnp.zeros((HALO_A, wc), F32)
        hist_b[0:HALO_B, :] = jnp.zeros((HALO_B, wc), F32)
        hist_c[0:HALO_C, :] = jnp.zeros((HALO_C, wc), F32)

    x = x_ref[...]
    mod = mod_ref[...]
    hb = _rms_mod(x, g1_ref[...], mod[:, d:2 * d], mod[:, 0:d]).astype(BF16)

    za = _bdot(hb, win_ref[:, 0:2 * wc])
    hist_a[HALO_A:HALO_A + tt, :] = za[:, :wc] * _sigmoid(za[:, wc:])
    base_a = HALO_A - (CONF_K - 1)
    chunks = []
    for r0 in range(0, tt, CONV_ROW_CHUNK):
        acc = None
        for k in range(CONF_K):
            term = hist_a[pl.ds(base_a + r0 + k, CONV_ROW_CHUNK), :] * cdw_ref[k:k + 1, :]
            acc = term if acc is None else acc + term
        chunks.append(acc)
    v_a = jnp.concatenate(chunks, axis=0) + cdwb_ref[...]
    y_a_in = _layernorm_silu(v_a, lng_ref[...], lnb_ref[...])

    u_b = _bdot(hb, win_ref[:, 2 * wc:3 * wc])
    hist_b[HALO_B:HALO_B + tt, :] = u_b
    pos = t * tt + lax.broadcasted_iota(jnp.int32, (tt, 1), 0)
    gw = wc // len(POOL_WINDOWS)
    pooled = []
    for gi, win in enumerate(POOL_WINDOWS):
        sl = slice(gi * gw, (gi + 1) * gw)
        s = u_b[:, sl]
        for j in range(1, win):
            s = s + hist_b[pl.ds(HALO_B - j, tt), sl]
        count = jnp.minimum(pos + 1, win).astype(F32)
        pooled.append(s / count - u_b[:, sl])
    pooled = jnp.concatenate(pooled, axis=-1)

    zs = _bdot(hb, win_ref[:, 3 * wc:6 * wc])
    s_x, s_b, s_c = zs[:, 0:wc], zs[:, wc:2 * wc], zs[:, 2 * wc:3 * wc]
    hist_c[HALO_C:HALO_C + tt, :] = s_c * s_x
    base_c = HALO_C - (SC_K - 1)
    v_c = None
    for k in range(SC_K):
        term = hist_c[pl.ds(base_c + k, tt), :] * sdw_ref[k:k + 1, :]
        v_c = term if v_c is None else v_c + term

    xo_ref[...] = _mixer_tail(x, mod, hb, y_a_in, pooled, s_b * v_c,
                              (win_ref, cpw_ref, pw_ref, ps_ref, sout_ref, wo_ref))

    @pl.when(t == pl.num_programs(1) - 1)
    def _():
        nconf_ref[...] = hist_a[pl.ds(HALO_A + tt - (CONF_K - 1), CONF_K - 1), :]
        npool_ref[...] = hist_b[pl.ds(HALO_B + tt - POOL_BUF, POOL_BUF), :]
        nsc_ref[...] = hist_c[pl.ds(HALO_C + tt - (SC_K - 1), SC_K - 1), :]

    hist_a[0:HALO_A, :] = hist_a[tt:tt + HALO_A, :]
    hist_b[0:HALO_B, :] = hist_b[tt:tt + HALO_B, :]
    hist_c[0:HALO_C, :] = hist_c[tt:tt + HALO_C, :]


def _const_spec(shape, index):
    return pl.BlockSpec(shape, index, pipeline_mode=pl.Buffered(1))


def _mixer_weight_specs(l, w, grid_rank):
    zeros = lambda n: (0,) * n
    specs = []
    for a in w:
        nd = a.ndim - 1
        idx = (lambda *_, nd=nd: (l,) + zeros(nd))
        specs.append(_const_spec((None,) + a.shape[1:], idx))
    return specs


def _mixer_prompt(l, x, mod4, row0, g1, w, tt):
    b, t, d = x.shape
    win, cdw, cdwb, lng, lnb, cpw, pw, ps, sdw, sout, wo = w
    wc = cdw.shape[-1]
    kern = functools.partial(_mixer_prompt_kernel, tt=tt)
    weights = (g1, win, cdw, cdwb, lng, lnb, cpw, pw, ps, sdw, sout, wo)
    return pl.pallas_call(
        kern,
        grid=(b, t // tt),
        in_specs=[
            pl.BlockSpec((None, tt, d), lambda i, j: (i, j, 0)),
            pl.BlockSpec((None, None, 1, 3 * d), lambda i, j: (l, row0 + i, 0, 0)),
        ] + _mixer_weight_specs(l, weights, 2),
        out_specs=[
            pl.BlockSpec((None, tt, d), lambda i, j: (i, j, 0)),
            pl.BlockSpec((None, CONF_K - 1, wc), lambda i, j: (i, 0, 0)),
            pl.BlockSpec((None, POOL_BUF, wc), lambda i, j: (i, 0, 0)),
            pl.BlockSpec((None, SC_K - 1, wc), lambda i, j: (i, 0, 0)),
        ],
        out_shape=[
            jax.ShapeDtypeStruct((b, t, d), F32),
            jax.ShapeDtypeStruct((b, CONF_K - 1, wc), F32),
            jax.ShapeDtypeStruct((b, POOL_BUF, wc), F32),
            jax.ShapeDtypeStruct((b, SC_K - 1, wc), F32),
        ],
        scratch_shapes=[
            pltpu.VMEM((HALO_A + tt, wc), F32),
            pltpu.VMEM((HALO_B + tt, wc), F32),
            pltpu.VMEM((HALO_C + tt, wc), F32),
        ],
        compiler_params=pltpu.CompilerParams(
            dimension_semantics=("arbitrary", "arbitrary"),
            vmem_limit_bytes=V7X_VMEM_LIMIT_BYTES),
        name=f"mixer_prompt_l{l}",
    )(x, mod4, *weights)


def _mixer_sample_kernel(x_ref, mod_ref, sta_ref, stb_ref, stc_ref, g1_ref, win_ref, cdw_ref,
                         cdwb_ref, lng_ref, lnb_ref, cpw_ref, pw_ref, ps_ref, sdw_ref, sout_ref,
                         wo_ref, xo_ref, nconf_ref, npool_ref, nsc_ref):
    d = x_ref.shape[-1]
    wc = sta_ref.shape[-1]
    x = x_ref[...]
    mod = mod_ref[...]
    hb = _rms_mod(x, g1_ref[...], mod[:, d:2 * d], mod[:, 0:d]).astype(BF16)

    za = _bdot(hb, win_ref[:, 0:2 * wc])
    u_a = za[:, :wc] * _sigmoid(za[:, wc:])
    v_a = u_a * cdw_ref[CONF_K - 1:CONF_K, :]
    for k in range(CONF_K - 1):
        v_a = v_a + sta_ref[:, k, :] * cdw_ref[k:k + 1, :]
    y_a_in = _layernorm_silu(v_a + cdwb_ref[...], lng_ref[...], lnb_ref[...])
    for k in range(CONF_K - 2):
        nconf_ref[:, k, :] = sta_ref[:, k + 1, :]
    nconf_ref[:, CONF_K - 2, :] = u_a

    u_b = _bdot(hb, win_ref[:, 2 * wc:3 * wc])
    gw = wc // len(POOL_WINDOWS)
    pooled = []
    for gi, win in enumerate(POOL_WINDOWS):
        sl = slice(gi * gw, (gi + 1) * gw)
        s = u_b[:, sl]
        for j in range(1, win):
            s = s + stb_ref[:, POOL_BUF - j, sl]
        count = float(min(PAST_LEN + 1, win))
        pooled.append(s / count - u_b[:, sl])
    pooled = jnp.concatenate(pooled, axis=-1)
    for k in range(POOL_BUF - 1):
        npool_ref[:, k, :] = stb_ref[:, k + 1, :]
    npool_ref[:, POOL_BUF - 1, :] = u_b

    zs = _bdot(hb, win_ref[:, 3 * wc:6 * wc])
    s_x, s_b, s_c = zs[:, 0:wc], zs[:, wc:2 * wc], zs[:, 2 * wc:3 * wc]
    gated = s_c * s_x
    v_c = gated * sdw_ref[SC_K - 1:SC_K, :]
    for k in range(SC_K - 1):
        v_c = v_c + stc_ref[:, k, :] * sdw_ref[k:k + 1, :]
    for k in range(SC_K - 2):
        nsc_ref[:, k, :] = stc_ref[:, k + 1, :]
    nsc_ref[:, SC_K - 2, :] = gated

    xo_ref[...] = _mixer_tail(x, mod, hb, y_a_in, pooled, s_b * v_c,
                              (win_ref, cpw_ref, pw_ref, ps_ref, sout_ref, wo_ref))


def _mixer_sample(l, x, mod, sta, stb, stc, g1, w):
    n, d = x.shape
    win, cdw, cdwb, lng, lnb, cpw, pw, ps, sdw, sout, wo = w
    wc = cdw.shape[-1]
    weights = (g1, win, cdw, cdwb, lng, lnb, cpw, pw, ps, sdw, sout, wo)
    state_spec = lambda a: pl.BlockSpec((None,) + a.shape[1:], lambda i: (l, 0, 0, 0))
    return pl.pallas_call(
        _mixer_sample_kernel,
        grid=(1,),
        in_specs=[
            pl.BlockSpec((n, d), lambda i: (0, 0)),
            pl.BlockSpec((None, n, 3 * d), lambda i: (l, 0, 0)),
            state_spec(sta), state_spec(stb), state_spec(stc),
        ] + _mixer_weight_specs(l, weights, 1),
        out_specs=[
            pl.BlockSpec((n, d), lambda i: (0, 0)),
            pl.BlockSpec((n, CONF_K - 1, wc), lambda i: (0, 0, 0)),
            pl.BlockSpec((n, POOL_BUF, wc), lambda i: (0, 0, 0)),
            pl.BlockSpec((n, SC_K - 1, wc), lambda i: (0, 0, 0)),
        ],
        out_shape=[
            jax.ShapeDtypeStruct((n, d), F32),
            jax.ShapeDtypeStruct((n, CONF_K - 1, wc), F32),
            jax.ShapeDtypeStruct((n, POOL_BUF, wc), F32),
            jax.ShapeDtypeStruct((n, SC_K - 1, wc), F32),
        ],
        compiler_params=pltpu.CompilerParams(
            dimension_semantics=("arbitrary",),
            vmem_limit_bytes=V7X_VMEM_LIMIT_BYTES),
        name=f"mixer_sample_l{l}",
    )(x, mod, sta, stb, stc, *weights)


def _route(logits):
    lane = lax.broadcasted_iota(jnp.int32, logits.shape, 1)
    neg = -jnp.inf
    is_g = lane < N_GROUPS
    gl = jnp.where(is_g, logits, neg)
    gmax = jnp.max(gl, axis=-1, keepdims=True)
    gsum = jnp.sum(jnp.where(is_g, jnp.exp(gl - gmax), 0.0), axis=-1, keepdims=True)
    gval = 1.0 / gsum
    gidx = jnp.min(jnp.where(gl == gmax, lane, LANES), axis=-1, keepdims=True)
    in_grp = (lane >= N_GROUPS) & (lane < N_GROUPS + N_EXPERTS) & (
        (lane - N_GROUPS) // EXP_PER_GROUP == gidx)
    sel = jnp.where(in_grp, logits, neg)
    smax = jnp.max(sel, axis=-1, keepdims=True)
    sexp = jnp.where(in_grp, jnp.exp(sel - smax), 0.0)
    p = sexp / jnp.sum(sexp, axis=-1, keepdims=True)
    p1 = jnp.max(jnp.where(in_grp, p, -1.0), axis=-1, keepdims=True)
    i1 = jnp.min(jnp.where(in_grp & (p == p1), lane, LANES), axis=-1, keepdims=True)
    rest = in_grp & (lane != i1)
    p2 = jnp.max(jnp.where(rest, p, -1.0), axis=-1, keepdims=True)
    i2 = jnp.min(jnp.where(rest & (p == p2), lane, LANES), axis=-1, keepdims=True)
    tsum = p1 + p2
    return i1, i2, gval * (p1 / tsum), gval * (p2 / tsum)


def _pack_bf16_pairs(v):
    n = v.shape[-1] // 2
    hi = lax.bitcast_convert_type(v[:, :n].astype(BF16).astype(F32), jnp.uint32)
    lo = lax.bitcast_convert_type(v[:, n:].astype(BF16).astype(F32), jnp.uint32)
    return lax.bitcast_convert_type((hi & jnp.uint32(0xFFFF0000)) | (lo >> 16), jnp.int32)


def _unpack_bf16_pairs(w):
    u = lax.bitcast_convert_type(w, jnp.uint32)
    hi = lax.bitcast_convert_type(u & jnp.uint32(0xFFFF0000), F32)
    lo = lax.bitcast_convert_type(u << 16, F32)
    return hi, lo


def _moe_dense_kernel(x_ref, mod_ref, g2_ref, rw_ref, rb_ref, w1_ref, w3_ref, w2_ref, fg_ref,
                      o_ref, hb_ref, comb_ref, acc_ref, *, final_norm):
    e = pl.program_id(2)
    d = x_ref.shape[-1]

    @pl.when(e == 0)
    def _():
        mod = mod_ref[...]
        h = _rms_mod(x_ref[...], g2_ref[...], mod[:, d:2 * d], mod[:, 0:d])
        hb_ref[...] = h.astype(BF16)
        logits = jnp.dot(h, rw_ref[...], precision=lax.Precision.HIGHEST,
                         preferred_element_type=F32) + rb_ref[...]
        i1, i2, c1, c2 = _route(logits)
        lane = lax.broadcasted_iota(jnp.int32, logits.shape, 1)
        comb_ref[...] = jnp.where(lane == i1, c1, jnp.where(lane == i2, c2, 0.0))
        acc_ref[...] = jnp.zeros(acc_ref.shape, F32)

    hb = hb_ref[...]
    a = _bdot(hb, w1_ref[...])
    act = (a * _sigmoid(a)) * _bdot(hb, w3_ref[...])
    y = _bdot(act.astype(BF16), w2_ref[...])
    comb = comb_ref[...]
    lane = lax.broadcasted_iota(jnp.int32, comb.shape, 1)
    ce = jnp.sum(jnp.where(lane == e + N_GROUPS, comb, 0.0), axis=-1, keepdims=True)
    acc_ref[...] += y * ce

    @pl.when(e == pl.num_programs(2) - 1)
    def _():
        xn = x_ref[...] + mod_ref[:, 2 * d:3 * d] * acc_ref[...]
        if final_norm:
            ms = jnp.mean(xn * xn, axis=-1, keepdims=True)
            xn = xn * lax.rsqrt(ms + EPS) * fg_ref[...]
        o_ref[...] = xn


def _moe_dense(l, x, mod_spec, mod_arr, g2, rw, rb, w1, w3, w2, fg, tm, final_norm):
    b, t, d = x.shape
    ne, _, dff = w1.shape[1:]
    kern = functools.partial(_moe_dense_kernel, final_norm=final_norm)
    return pl.pallas_call(
        kern,
        grid=(b, t // tm, ne),
        in_specs=[
            pl.BlockSpec((None, tm, d), lambda i, j, e: (i, j, 0)),
            mod_spec,
            _const_spec((None, 1, d), lambda i, j, e: (l, 0, 0)),
            _const_spec((None, d, LANES), lambda i, j, e: (l, 0, 0)),
            _const_spec((None, 1, LANES), lambda i, j, e: (l, 0, 0)),
            pl.BlockSpec((None, None, d, dff), lambda i, j, e: (l, e, 0, 0)),
            pl.BlockSpec((None, None, d, dff), lambda i, j, e: (l, e, 0, 0)),
            pl.BlockSpec((None, None, dff, d), lambda i, j, e: (l, e, 0, 0)),
            _const_spec((1, d), lambda i, j, e: (0, 0)),
        ],
        out_specs=pl.BlockSpec((None, tm, d), lambda i, j, e: (i, j, 0)),
        out_shape=jax.ShapeDtypeStruct((b, t, d), F32),
        scratch_shapes=[
            pltpu.VMEM((tm, d), BF16),
            pltpu.VMEM((tm, LANES), F32),
            pltpu.VMEM((tm, d), F32),
        ],
        compiler_params=pltpu.CompilerParams(
            dimension_semantics=("arbitrary", "arbitrary", "arbitrary"),
            vmem_limit_bytes=V7X_VMEM_LIMIT_BYTES),
        name=f"moe_dense_l{l}_{'prompt' if b > 1 else 'sample'}",
    )(x, mod_arr, g2, rw, rb, w1, w3, w2, fg)


V7X_SC_CORES = 2
V7X_SC_SUBCORES = 16
SC_GATHER_WINDOW = 64
MOE_CHUNK_ROWS = 256
ROUTE_TOKEN_TILE = 512
COMBINE_TOKEN_TILE = 512


def _moe_route_kernel(x_ref, mod_ref, g2_ref, rw_ref, rb_ref,
                      hp_ref, ri_ref, rf_ref, cnt_ref, carry_ref):
    d = x_ref.shape[-1]
    tm = x_ref.shape[0]
    first = (pl.program_id(0) == 0) & (pl.program_id(1) == 0)

    @pl.when(first)
    def _():
        carry_ref[...] = jnp.zeros(carry_ref.shape, F32)

    mod = mod_ref[...]
    h = _rms_mod(x_ref[...], g2_ref[...], mod[:, d:2 * d], mod[:, 0:d])
    hp_ref[...] = _pack_bf16_pairs(h)
    logits = jnp.dot(h, rw_ref[...], precision=lax.Precision.HIGHEST,
                     preferred_element_type=F32) + rb_ref[...]
    i1, i2, c1, c2 = _route(logits)
    lane = lax.broadcasted_iota(jnp.int32, logits.shape, 1)
    chosen = (lane == i1) | (lane == i2)
    r_io = lax.broadcasted_iota(jnp.int32, (tm, tm), 0)
    c_io = lax.broadcasted_iota(jnp.int32, (tm, tm), 1)
    before = _bdot((c_io < r_io).astype(BF16), chosen.astype(BF16)) + carry_ref[...]
    r1 = jnp.sum(jnp.where(lane == i1, before, 0.0), axis=-1, keepdims=True).astype(jnp.int32)
    r2 = jnp.sum(jnp.where(lane == i2, before, 0.0), axis=-1, keepdims=True).astype(jnp.int32)
    carry_ref[...] += jnp.sum(chosen.astype(F32), axis=0, keepdims=True)
    cnt_ref[...] = carry_ref[...]
    zero_i = jnp.zeros(lane.shape, jnp.int32)
    ri_ref[...] = jnp.where(lane == 0, i1 - N_GROUPS, jnp.where(
        lane == 1, i2 - N_GROUPS, jnp.where(lane == 2, r1, jnp.where(lane == 3, r2, zero_i))))
    rf_ref[...] = jnp.where(lane == 0, c1, jnp.where(lane == 1, c2, 0.0))


def _moe_route(l, x, mod4, row0, g2, rw, rb, tm):
    b, t, d = x.shape
    n = b * t
    nt = t // tm
    tok = lambda w: pl.BlockSpec((tm, w), lambda i, j: (i * nt + j, 0))
    return pl.pallas_call(
        _moe_route_kernel,
        grid=(b, nt),
        in_specs=[
            pl.BlockSpec((None, tm, d), lambda i, j: (i, j, 0)),
            pl.BlockSpec((None, None, 1, 3 * d), lambda i, j: (l, row0 + i, 0, 1)),
            _const_spec((None, 1, d), lambda i, j: (l, 0, 0)),
            _const_spec((None, d, LANES), lambda i, j: (l, 0, 0)),
            _const_spec((None, 1, LANES), lambda i, j: (l, 0, 0)),
        ],
        out_specs=[tok(d // 2), tok(LANES), tok(LANES),
                   pl.BlockSpec((1, LANES), lambda i, j: (0, 0))],
        out_shape=[
            jax.ShapeDtypeStruct((n, d // 2), jnp.int32),
            jax.ShapeDtypeStruct((n, LANES), jnp.int32),
            jax.ShapeDtypeStruct((n, LANES), F32),
            jax.ShapeDtypeStruct((1, LANES), F32),
        ],
        scratch_shapes=[pltpu.VMEM((1, LANES), F32)],
        compiler_params=pltpu.CompilerParams(
            dimension_semantics=("arbitrary", "arbitrary"),
            vmem_limit_bytes=V7X_VMEM_LIMIT_BYTES),
        name=f"moe_route_l{l}",
    )(x, mod4, g2, rw, rb)


def _sc_row_gather(table, idx, window):
    n_out = idx.shape[0]
    d = table.shape[1]
    nw = V7X_SC_CORES * V7X_SC_SUBCORES
    per_w = n_out // nw
    nwin = per_w // window
    assert per_w * nw == n_out and nwin * window == per_w and nwin % 2 == 0 and per_w % 8 == 0
    mesh = plsc.VectorSubcoreMesh(core_axis_name="c", subcore_axis_name="s",
                                  num_cores=V7X_SC_CORES, num_subcores=V7X_SC_SUBCORES)

    @functools.partial(
        pl.kernel, mesh=mesh,
        out_type=jax.ShapeDtypeStruct((n_out, d), table.dtype),
        scratch_types=[pltpu.VMEM((per_w,), jnp.int32),
                       pltpu.VMEM((window, d), table.dtype), pltpu.VMEM((window, d), table.dtype),
                       pltpu.SemaphoreType.DMA, pltpu.SemaphoreType.DMA,
                       pltpu.SemaphoreType.DMA, pltpu.SemaphoreType.DMA],
    )
    def gather_kernel(table_hbm, idx_hbm, out_hbm, idx_v, rows0, rows1, g0, g1, w0, w1):
        wid = lax.axis_index("s") * V7X_SC_CORES + lax.axis_index("c")
        base = pl.multiple_of(wid * per_w, 8)
        pltpu.sync_copy(idx_hbm.at[pl.ds(base, per_w)], idx_v)
        rows, gsem, wsem = (rows0, rows1), (g0, g1), (w0, w1)

        def gather(j, b):
            win = idx_v.at[pl.ds(pl.multiple_of(j * window, 8), window)]
            return pltpu.make_async_copy(table_hbm.at[win], rows[b], gsem[b])

        def writeback(j, b):
            off = pl.multiple_of(base + j * window, 8)
            return pltpu.make_async_copy(rows[b], out_hbm.at[pl.ds(off, window)], wsem[b])

        gather(0, 0).start()

        @pl.loop(0, nwin, step=2)
        def _(i):
            for b in (0, 1):
                j = i + b
                gather(j, b).wait()

                @pl.when(j >= 1)
                def _():
                    writeback(j - 1, 1 - b).wait()

                @pl.when(j + 1 < nwin)
                def _():
                    gather(j + 1, 1 - b).start()

                writeback(j, b).start()

        writeback(nwin - 1, 1).wait()

    return gather_kernel(table, idx)


def _moe_expert_kernel(ce_ref, nv_ref, hs_ref, w1_ref, w3_ref, w2_ref, ys_ref):
    c = pl.program_id(0)

    @pl.when(c < nv_ref[0])
    def _():
        ha, hb2 = _unpack_bf16_pairs(hs_ref[...])
        h = jnp.concatenate([ha.astype(BF16), hb2.astype(BF16)], axis=-1)
        a = _bdot(h, w1_ref[...])
        act = (a * _sigmoid(a)) * _bdot(h, w3_ref[...])
        ys_ref[...] = _pack_bf16_pairs(_bdot(act.astype(BF16), w2_ref[...]))

    @pl.when(c >= nv_ref[0])
    def _():
        ys_ref[...] = jnp.zeros(ys_ref.shape, jnp.int32)


def _moe_experts(l, hs, chunk_expert, n_valid, w1, w3, w2, rows):
    m, dh = hs.shape
    _, _, d, dff = w1.shape
    wspec = lambda a, b: pl.BlockSpec((None, None, a, b), lambda c, ce, nv: (l, ce[c], 0, 0))
    return pl.pallas_call(
        _moe_expert_kernel,
        grid_spec=pltpu.PrefetchScalarGridSpec(
            num_scalar_prefetch=2,
            grid=(m // rows,),
            in_specs=[pl.BlockSpec((rows, dh), lambda c, ce, nv: (c, 0)),
                      wspec(d, dff), wspec(d, dff), wspec(dff, d)],
            out_specs=pl.BlockSpec((rows, dh), lambda c, ce, nv: (c, 0)),
        ),
        out_shape=jax.ShapeDtypeStruct((m, dh), jnp.int32),
        compiler_params=pltpu.CompilerParams(
            dimension_semantics=("arbitrary",),
            vmem_limit_bytes=V7X_VMEM_LIMIT_BYTES),
        name=f"moe_experts_l{l}",
    )(chunk_expert, n_valid, hs, w1, w3, w2)


def _moe_combine_kernel(x_ref, mod_ref, rf_ref, y0_ref, y1_ref, fg_ref, o_ref, *, final_norm):
    rf = rf_ref[...]
    c1, c2 = rf[:, 0:1], rf[:, 1:2]
    a0, b0 = _unpack_bf16_pairs(y0_ref[...])
    a1, b1 = _unpack_bf16_pairs(y1_ref[...])
    moe = jnp.concatenate([a0 * c1 + a1 * c2, b0 * c1 + b1 * c2], axis=-1)
    xn = x_ref[...] + mod_ref[...] * moe
    if final_norm:
        ms = jnp.mean(xn * xn, axis=-1, keepdims=True)
        xn = xn * lax.rsqrt(ms + EPS) * fg_ref[...]
    o_ref[...] = xn


def _moe_combine(l, x, mod4, row0, rf, yg, fg, tm, final_norm):
    b, t, d = x.shape
    nt = t // tm
    nblk = b * nt
    kern = functools.partial(_moe_combine_kernel, final_norm=final_norm)
    return pl.pallas_call(
        kern,
        grid=(b, nt),
        in_specs=[
            pl.BlockSpec((None, tm, d), lambda i, j: (i, j, 0)),
            pl.BlockSpec((None, None, 1, d), lambda i, j: (l, row0 + i, 0, 5)),
            pl.BlockSpec((tm, LANES), lambda i, j: (i * nt + j, 0)),
            pl.BlockSpec((tm, d // 2), lambda i, j: (i * nt + j, 0)),
            pl.BlockSpec((tm, d // 2), lambda i, j: (nblk + i * nt + j, 0)),
            _const_spec((1, d), lambda i, j: (0, 0)),
        ],
        out_specs=pl.BlockSpec((None, tm, d), lambda i, j: (i, j, 0)),
        out_shape=jax.ShapeDtypeStruct((b, t, d), F32),
        compiler_params=pltpu.CompilerParams(
            dimension_semantics=("arbitrary", "arbitrary"),
            vmem_limit_bytes=V7X_VMEM_LIMIT_BYTES),
        name=f"moe_combine_l{l}",
    )(x, mod4, rf, yg, yg, fg)


def _moe_sparse(l, x, mod4, row0, g2, rw, rb, w1, w3, w2, fg, final_norm):
    b, t, d = x.shape
    n = b * t
    rows = MOE_CHUNK_ROWS
    m = 2 * n + N_EXPERTS * rows
    hp, ri, rf, cnt = _moe_route(l, x, mod4, row0, g2, rw, rb, ROUTE_TOKEN_TILE)

    counts = cnt[0, N_GROUPS:N_GROUPS + N_EXPERTS].astype(jnp.int32)
    padded = (counts + rows - 1) // rows * rows
    ends = jnp.cumsum(padded)
    starts = ends - padded
    pos = jnp.take(starts, ri[:, 0:2], axis=0) + ri[:, 2:4]
    pos_flat = jnp.concatenate([pos[:, 0], pos[:, 1]])
    tok = jnp.arange(n, dtype=jnp.int32)
    src = (jnp.arange(m, dtype=jnp.int32) % n).at[pos_flat].set(jnp.concatenate([tok, tok]))
    chunk_start = jnp.arange(m // rows, dtype=jnp.int32) * rows
    chunk_expert = jnp.minimum(
        jnp.sum(ends[None, :] <= chunk_start[:, None], axis=1), N_EXPERTS - 1).astype(jnp.int32)
    n_valid = (ends[-1:] // rows).astype(jnp.int32)

    hs = _sc_row_gather(hp, src, SC_GATHER_WINDOW)
    ys = _moe_experts(l, hs, chunk_expert, n_valid, w1, w3, w2, rows)
    yg = _sc_row_gather(ys, pos_flat, SC_GATHER_WINDOW)
    return _moe_combine(l, x, mod4, row0, rf, yg, fg, COMBINE_TOKEN_TILE, final_norm)


MIXER_TIME_TILE = 256


def kernel(x_prompt, x_sample, state_conf, state_pool, state_sconv, c_prompt, c_sample, w_ada, b_ada, norm1_g, norm2_g, final_g, w_in, conf_dw, conf_dw_b, conf_ln_g, conf_ln_b, conf_pw, pool_w, pool_scale, sc_dw, sc_out, w_o, router_g, router_g_b, router_e, router_e_b, exp_w1, exp_w3, exp_w2):
    depth, d, _ = w_ada.shape
    nb, seq, _ = x_prompt.shape
    ns = x_sample.shape[0]

    c_all = jnp.concatenate([c_sample, c_prompt], axis=0)
    mod = _modulation(c_all, w_ada, b_ada)
    mod4 = mod.reshape(depth, ns + nb, 1, 6 * d)

    row = lambda a: a.reshape(depth, 1, a.shape[-1])
    mixer_w = (w_in.astype(BF16), conf_dw, row(conf_dw_b), row(conf_ln_g), row(conf_ln_b),
               conf_pw.astype(BF16), pool_w.astype(BF16), row(pool_scale), sc_dw,
               sc_out.astype(BF16), w_o.astype(BF16))
    g1, g2 = row(norm1_g), row(norm2_g)
    fg = final_g.reshape(1, d)
    pad = LANES - N_GROUPS - N_EXPERTS
    rw = jnp.pad(jnp.concatenate([router_g, router_e], axis=-1), ((0, 0), (0, 0), (0, pad)))
    rb = jnp.pad(jnp.concatenate([router_g_b, router_e_b], axis=-1), ((0, 0), (0, pad)))
    rb = rb.reshape(depth, 1, LANES)
    w1, w3, w2 = exp_w1.astype(BF16), exp_w3.astype(BF16), exp_w2.astype(BF16)

    xp = x_prompt
    xs = x_sample.reshape(ns, d)
    new_p = [[], [], []]
    new_s = [[], [], []]
    for l in range(depth):
        last = l == depth - 1
        xp, na, nbuf, nc = _mixer_prompt(l, xp, mod4, ns, g1, mixer_w, MIXER_TIME_TILE)
        for dst, v in zip(new_p, (na, nbuf, nc)):
            dst.append(v)
        xp = _moe_sparse(l, xp, mod4, ns, g2, rw, rb, w1, w3, w2, fg, last)
        xs, na, nbuf, nc = _mixer_sample(l, xs, mod, state_conf, state_pool, state_sconv,
                                         g1, mixer_w)
        for dst, v in zip(new_s, (na, nbuf, nc)):
            dst.append(v)
        mod_spec_s = pl.BlockSpec((None, ns, 3 * d), lambda i, j, e, l=l: (l, 0, 1))
        xs = _moe_dense(l, xs.reshape(1, ns, d), mod_spec_s, mod, g2, rw, rb, w1, w3, w2, fg,
                        ns, last).reshape(ns, d)

    stack = lambda parts: jnp.stack(parts, axis=0)
    return (xp, xs.reshape(ns, 1, d),
            stack(new_p[0]), stack(new_s[0]),
            stack(new_p[1]), stack(new_s[1]),
            stack(new_p[2]), stack(new_s[2]))
```
